```python
import jax, jax.numpy as jnp
from jax import lax
import numpy as np

D_MODEL = 1024
BATCH = 4
SEQ = 4096
DEPTH = 4
DEC_BATCH = 128
DEC_SEQ = 4
PAST_LEN = 2048
PAGE_SIZE = 128

BRANCH_W = D_MODEL // 2
N_BRANCH = 3
POOL_GROUPS = 4
POOL_WINDOWS = (2, 4, 8, 16)
POOL_GW = BRANCH_W // POOL_GROUPS
POOL_BUF = max(POOL_WINDOWS) - 1
SB_HEAD_DIM = 64
SB_HEADS = BRANCH_W // SB_HEAD_DIM
SB_SCALE = SB_HEAD_DIM ** -0.5
SB_Q_BLOCK = 128
SB_BIAS_INIT = -8.0
HG_KEY = 128
HG_HEADS = BRANCH_W // HG_KEY
HG_VAL = BRANCH_W // HG_HEADS
HG_CHUNK = 64
N_EXPERTS = 32
TOP_K = 4
D_FF = D_MODEL
SWIGLU_LIMIT = 7.0
SWIGLU_ALPHA = 1.702
MOE_BLOCK = 128
EPS = 1e-6
D_IN = 8 * BRANCH_W + N_BRANCH * D_MODEL

kernel_name = 'hybrid_pool_stickbreak_hgrn2_moe_step'

F32 = jnp.float32


def rmsnorm(x, g):
    xf = x.astype(F32)
    y = xf * lax.rsqrt(jnp.mean(xf * xf, axis=-1, keepdims=True) + EPS)
    return (y * g.astype(F32)).astype(x.dtype)


def adaln(c, w_ada, b_ada):
    m = jax.nn.silu(c) @ w_ada + b_ada
    return [t[:, None, :] for t in jnp.split(m, 6, axis=-1)]


def pool_branch(u_ext, n_prev, pos0, w_pool, pool_scale):
    B, L, _ = u_ext.shape
    T = L - n_prev
    uf = u_ext.astype(F32)
    cs = jnp.pad(jnp.cumsum(uf, axis=1), ((0, 0), (1, 0), (0, 0)))
    end = n_prev + 1 + jnp.arange(T)
    pos = (pos0 + jnp.arange(T)).astype(F32)
    x_tok = uf[:, n_prev:]
    groups = []
    for g, w in enumerate(POOL_WINDOWS):
        csg = cs[..., g * POOL_GW:(g + 1) * POOL_GW]
        wsum = csg[:, end] - csg[:, jnp.maximum(end - w, 0)]
        count = jnp.minimum(float(w), pos + 1.0)
        groups.append(wsum / count[None, :, None] - x_tok[..., g * POOL_GW:(g + 1) * POOL_GW])
    pooled = jnp.stack(groups, axis=2)
    mixed = jnp.einsum('btgc,gcd->btgd', pooled, w_pool.astype(F32))
    return mixed.reshape(B, T, BRANCH_W) * pool_scale.astype(F32)


def stick_breaking(q, k, v, q_pos, k_pos, bias):
    z = jnp.einsum('bqhd,bkhd->bhqk', q, k) * SB_SCALE + bias.astype(F32)[None, :, None, None]
    mask = (k_pos[None, :] < q_pos[:, None])[None, None]
    log_keep = jnp.where(mask, jax.nn.log_sigmoid(-z), 0.0)
    tail = lax.cumsum(log_keep, axis=3, reverse=True)
    between = jnp.concatenate([tail[..., 1:], jnp.zeros_like(tail[..., :1])], axis=-1)
    log_a = jnp.where(mask, jax.nn.log_sigmoid(z) + between, -jnp.inf)
    return jnp.einsum('bhqk,bkhd->bqhd', jnp.exp(log_a), v)


def sb_prompt(q, k, v, bias):
    B, T, H, Dh = q.shape
    nb = T // SB_Q_BLOCK
    qb = q.reshape(B, nb, SB_Q_BLOCK, H, Dh).swapaxes(0, 1)
    pos = jnp.arange(T)
    pb = pos.reshape(nb, SB_Q_BLOCK)
    out = lax.map(lambda a: stick_breaking(a[0], k, v, a[1], pos, bias), (qb, pb))
    return out.swapaxes(0, 1).reshape(B, T, H, Dh)


def hgrn_prep(q_hg, f_hg, i_hg, lb):
    lead = q_hg.shape[:-1]
    q = jax.nn.silu(q_hg.astype(F32)).reshape(*lead, HG_HEADS, HG_KEY)
    zf = f_hg.astype(F32).reshape(*lead, HG_HEADS, HG_KEY)
    lbh = lb.reshape(HG_HEADS, HG_KEY)
    kk = (1.0 - lbh) * jax.nn.sigmoid(-zf)
    logf = jnp.logaddexp(jnp.log(lbh), jnp.log1p(-lbh) + jax.nn.log_sigmoid(zf))
    i = i_hg.astype(F32).reshape(*lead, HG_HEADS, HG_VAL)
    return q, logf, kk, i


def hgrn_chunk(S, q, logf, kk, i):
    C = q.shape[1]
    b = jnp.cumsum(logf, axis=1)
    o_inter = jnp.einsum('bthk,bhkv->bthv', q * jnp.exp(b), S)
    causal = jnp.tril(jnp.ones((C, C), bool))
    diff = b[:, :, None] - b[:, None, :]
    decay = jnp.exp(jnp.where(causal[None, :, :, None, None], diff, -jnp.inf))
    att = jnp.einsum('bthk,bshk,btshk->bhts', q, kk, decay)
    o_intra = jnp.einsum('bhts,bshv->bthv', att, i)
    b_last = b[:, -1]
    S_new = jnp.exp(b_last)[..., None] * S + jnp.einsum('bshk,bshv->bhkv', kk * jnp.exp(b_last[:, None] - b), i)
    return S_new, o_inter + o_intra


def hgrn_scan(q, logf, kk, i):
    B, T, H, K = q.shape
    nc = T // HG_CHUNK
    to_chunks = lambda t: t.reshape(B, nc, HG_CHUNK, *t.shape[2:]).swapaxes(0, 1)
    S0 = jnp.zeros((B, H, K, HG_VAL), F32)
    S_fin, o = lax.scan(lambda S, xs: hgrn_chunk(S, *xs), S0,
                        (to_chunks(q), to_chunks(logf), to_chunks(kk), to_chunks(i)))
    return S_fin, o.swapaxes(0, 1).reshape(B, T, H, HG_VAL)


def mixer(h, past, w_in, w_pool, pool_scale, sb_bias, lb, hg_norm, w_branch, w_out):
    B, T, _ = h.shape
    W = BRANCH_W
    z = h @ w_in
    u, q_sb, k_sb, v_sb, q_hg, f_hg, i_hg, g_hg = (z[..., j * W:(j + 1) * W] for j in range(8))
    gates = z[..., 8 * W:].reshape(B, T, N_BRANCH, D_MODEL)
    q_sb, k_sb, v_sb = (t.astype(F32).reshape(B, T, SB_HEADS, SB_HEAD_DIM) for t in (q_sb, k_sb, v_sb))
    hq, logf, kk, hi = hgrn_prep(q_hg, f_hg, i_hg, lb)
    if past is None:
        u_ext, n_prev, pos0 = u, 0, 0
        o_sb = sb_prompt(q_sb, k_sb, v_sb, sb_bias)
        S_new, o_hg = hgrn_scan(hq, logf, kk, hi)
    else:
        k_past, v_past, pool_buf, S0 = past
        n_prev, pos0 = pool_buf.shape[1], k_past.shape[1]
        u_ext = jnp.concatenate([pool_buf.astype(u.dtype), u], axis=1)
        k_all = jnp.concatenate([k_past.astype(F32), k_sb], axis=1)
        v_all = jnp.concatenate([v_past.astype(F32), v_sb], axis=1)
        o_sb = stick_breaking(q_sb, k_all, v_all, pos0 + jnp.arange(T), jnp.arange(pos0 + T), sb_bias)
        S_new, o_hg = hgrn_chunk(S0.astype(F32), hq, logf, kk, hi)
    o_pool = pool_branch(u_ext, n_prev, pos0, w_pool, pool_scale)
    o_hg = rmsnorm(o_hg, hg_norm).reshape(B, T, W) * jax.nn.silu(g_hg.astype(F32))
    branches = jnp.stack([o_pool, o_sb.reshape(B, T, W), o_hg], axis=2).astype(h.dtype)
    proj = jnp.einsum('btnc,ncd->btnd', branches, w_branch)
    merged = jnp.sum(jax.nn.sigmoid(gates.astype(F32)) * proj.astype(F32), axis=2).astype(h.dtype)
    return merged @ w_out, (k_sb, v_sb, u_ext[:, -POOL_BUF:], S_new)


def moe(h, w_router, b_router, w_gu, b_gu, w_dn, b_dn):
    B, T, D = h.shape
    xt = h.reshape(B * T, D)
    n_tok = B * T
    n_assign = n_tok * TOP_K
    logits = (xt @ w_router + b_router).astype(F32)
    top_val, top_idx = lax.top_k(logits, TOP_K)
    weight = jax.nn.softmax(top_val, axis=-1)
    e_flat = top_idx.reshape(-1)
    order = jnp.argsort(e_flat, stable=True)
    e_sorted = e_flat[order]
    tok_sorted = (order // TOP_K).astype(jnp.int32)
    w_sorted = weight.reshape(-1)[order]
    counts = jnp.bincount(e_flat, length=N_EXPERTS)
    padded = (counts + MOE_BLOCK - 1) // MOE_BLOCK * MOE_BLOCK
    pad_end = jnp.cumsum(padded)
    pad_start = pad_end - padded
    start = jnp.cumsum(counts) - counts
    dest = pad_start[e_sorted] + jnp.arange(n_assign) - start[e_sorted]
    n_rows = -(-(n_assign + N_EXPERTS * (MOE_BLOCK - 1)) // MOE_BLOCK) * MOE_BLOCK
    n_blocks = n_rows // MOE_BLOCK
    row_tok = jnp.zeros((n_rows,), jnp.int32).at[dest].set(tok_sorted)
    x_rows = xt[row_tok].reshape(n_blocks, MOE_BLOCK, D)
    blk_expert = jnp.minimum(jnp.searchsorted(pad_end, jnp.arange(n_blocks) * MOE_BLOCK, side='right'), N_EXPERTS - 1)

    def expert_block(args):
        xb, e = args
        gu = xb @ w_gu[e] + b_gu[e]
        gate = jnp.minimum(gu[..., :D_FF], SWIGLU_LIMIT)
        up = jnp.clip(gu[..., D_FF:], -SWIGLU_LIMIT, SWIGLU_LIMIT)
        act = (up + 1.0) * gate * jax.nn.sigmoid(SWIGLU_ALPHA * gate)
        return act @ w_dn[e] + b_dn[e]

    y_rows = lax.map(expert_block, (x_rows, blk_expert)).reshape(n_rows, D)
    contrib = y_rows[dest].astype(F32) * w_sorted[:, None]
    out = jnp.zeros((n_tok, D), F32).at[tok_sorted].add(contrib)
    return out.astype(h.dtype).reshape(B, T, D)


def trunk_layer(x, c, past, lw):
    (g_mix_l, g_ffn_l, w_ada_l, b_ada_l, w_in_l, w_pool_l, pool_scale_l, sb_bias_l, lb_l, hg_norm_l,
     w_branch_l, w_out_l, w_router_l, b_router_l, w_gu_l, b_gu_l, w_dn_l, b_dn_l) = lw
    sh1, sc1, ga1, sh2, sc2, ga2 = adaln(c, w_ada_l, b_ada_l)
    h = rmsnorm(x, g_mix_l) * (1.0 + sc1) + sh1
    mix, new_state = mixer(h, past, w_in_l, w_pool_l, pool_scale_l, sb_bias_l, lb_l, hg_norm_l, w_branch_l, w_out_l)
    x = x + ga1 * mix
    h = rmsnorm(x, g_ffn_l) * (1.0 + sc2) + sh2
    x = x + ga2 * moe(h, w_router_l, b_router_l, w_gu_l, b_gu_l, w_dn_l, b_dn_l)
    return x, new_state


def setup_inputs(seed: int = 0) -> dict:
    key = jax.random.key(seed)
    keys = iter(jax.random.split(key, 40))

    def nrm(shape, scale):
        return jax.random.normal(next(keys), shape, F32) * scale

    def gain(shape):
        return 1.0 + nrm(shape, 0.02)

    n_pages = PAST_LEN // PAGE_SIZE
    n_used = DEC_BATCH * n_pages
    n_phys = n_used + max(1, n_used // 4)
    return {
        'x_prompt': nrm((BATCH, SEQ, D_MODEL), 1.0),
        'x_sample': nrm((DEC_BATCH, DEC_SEQ, D_MODEL), 1.0),
        'cache_k': nrm((DEPTH, n_phys, PAGE_SIZE, SB_HEADS, SB_HEAD_DIM), 1.0),
        'cache_v': nrm((DEPTH, n_phys, PAGE_SIZE, SB_HEADS, SB_HEAD_DIM), 1.0),
        'state_pool': nrm((DEPTH, DEC_BATCH, POOL_BUF, BRANCH_W), 1.0),
        'state_hgrn': nrm((DEPTH, DEC_BATCH, HG_HEADS, HG_KEY, HG_VAL), 1.0),
        'page_table': jax.random.permutation(next(keys), n_phys)[:n_used].reshape(DEC_BATCH, n_pages).astype(jnp.int32),
        'c_prompt': nrm((BATCH, D_MODEL), 1.0),
        'c_sample': nrm((DEC_BATCH, D_MODEL), 1.0),
        'g_mix': gain((DEPTH, D_MODEL)),
        'g_ffn': gain((DEPTH, D_MODEL)),
        'g_final': gain((D_MODEL,)),
        'w_ada': nrm((DEPTH, D_MODEL, 6 * D_MODEL), 0.5 * D_MODEL ** -0.5),
        'b_ada': nrm((DEPTH, 6 * D_MODEL), 0.02),
        'w_in': nrm((DEPTH, D_MODEL, D_IN), D_MODEL ** -0.5),
        'w_pool': nrm((DEPTH, POOL_GROUPS, POOL_GW, POOL_GW), POOL_GW ** -0.5),
        'pool_scale': 1.0 + nrm((DEPTH, BRANCH_W), 0.1),
        'sb_bias': SB_BIAS_INIT + nrm((DEPTH, SB_HEADS), 0.5),
        'lb_logits': nrm((DEPTH, BRANCH_W), 0.5),
        'hg_norm': gain((DEPTH, HG_VAL)),
        'w_branch': nrm((DEPTH, N_BRANCH, BRANCH_W, D_MODEL), BRANCH_W ** -0.5),
        'w_out': nrm((DEPTH, D_MODEL, D_MODEL), D_MODEL ** -0.5),
        'w_router': nrm((DEPTH, D_MODEL, N_EXPERTS), D_MODEL ** -0.5),
        'b_router': nrm((DEPTH, N_EXPERTS), 0.01),
        'w_gu': nrm((DEPTH, N_EXPERTS, D_MODEL, 2 * D_FF), D_MODEL ** -0.5),
        'b_gu': nrm((DEPTH, N_EXPERTS, 2 * D_FF), 0.01),
        'w_dn': nrm((DEPTH, N_EXPERTS, D_FF, D_MODEL), D_FF ** -0.5),
        'b_dn': nrm((DEPTH, N_EXPERTS, D_MODEL), 0.01),
    }


def reference(x_prompt, x_sample, cache_k, cache_v, state_pool, state_hgrn, page_table, c_prompt, c_sample,
              g_mix, g_ffn, g_final, w_ada, b_ada, w_in, w_pool, pool_scale, sb_bias, lb_logits, hg_norm,
              w_branch, w_out, w_router, b_router, w_gu, b_gu, w_dn, b_dn):
    lb_all = jnp.cumsum(jax.nn.softmax(lb_logits.astype(F32), axis=0), axis=0)
    lb_all = jnp.maximum(lb_all - lb_all[0], 0.0)
    n_pages = page_table.shape[1]
    page = cache_k.shape[2]
    dec_b = x_sample.shape[0]
    xp, xs = x_prompt, x_sample
    kp_l, vp_l, pp_l, hp_l, ks_l, vs_l, ps_l, hs_l = ([] for _ in range(8))
    for l in range(DEPTH):
        lw = (g_mix[l], g_ffn[l], w_ada[l], b_ada[l], w_in[l], w_pool[l], pool_scale[l], sb_bias[l], lb_all[l],
              hg_norm[l], w_branch[l], w_out[l], w_router[l], b_router[l], w_gu[l], b_gu[l], w_dn[l], b_dn[l])
        xp, (k_new, v_new, buf_new, s_new) = trunk_layer(xp, c_prompt, None, lw)
        kp_l.append(k_new); vp_l.append(v_new); pp_l.append(buf_new); hp_l.append(s_new)
        k_past = cache_k[l][page_table].reshape(dec_b, n_pages * page, SB_HEADS, SB_HEAD_DIM)
        v_past = cache_v[l][page_table].reshape(dec_b, n_pages * page, SB_HEADS, SB_HEAD_DIM)
        xs, (k_new, v_new, buf_new, s_new) = trunk_layer(xs, c_sample, (k_past, v_past, state_pool[l], state_hgrn[l]), lw)
        ks_l.append(k_new); vs_l.append(v_new); ps_l.append(buf_new); hs_l.append(s_new)
    y_prompt = rmsnorm(xp, g_final)
    y_sample = rmsnorm(xs, g_final)
    return (y_prompt, y_sample,
            jnp.stack(kp_l), jnp.stack(vp_l), jnp.stack(pp_l), jnp.stack(hp_l),
            jnp.stack(ks_l), jnp.stack(vs_l), jnp.stack(ps_l), jnp.stack(hs_l))
```

```python
import functools

import jax
import jax.numpy as jnp
from jax import lax
from jax.experimental import pallas as pl
from jax.experimental.pallas import tpu as pltpu

F32 = jnp.float32
BF16 = jnp.bfloat16

D = 1024
W = 512
D_IN = 8 * W + 3 * D
SB_HEADS = 8
SB_SCALE = 0.125
HG_HEADS = 4
HK = 128
N_EXP = 32
TOP_K = 4
D_FF = 1024
SWIGLU_LIMIT = 7.0
SWIGLU_ALPHA = 1.702
EPS = 1e-6
POOL_WINDOWS = (2, 4, 8, 16)
POOL_BUF = 15
LANES = 128
MOE_TM = 256
NEG = -3.0e38

_NT = (((1,), (1,)), ((), ()))
_TN = (((0,), (0,)), ((), ()))


def _cp(sem, vmem_mb):
    return pltpu.CompilerParams(dimension_semantics=sem, vmem_limit_bytes=vmem_mb * 1024 * 1024)


def _softplus(z):
    return jnp.maximum(z, 0.0) + jnp.log1p(jnp.exp(-jnp.abs(z)))


def _silu(z):
    return z * jax.nn.sigmoid(z)


def _mod_spec(per_token, tm, tiles_per_b):
    if per_token:
        return pl.BlockSpec((None, tm, D), lambda i, *_: (0, i, 0))
    return pl.BlockSpec((None, 1, D), lambda i, *_: (i // tiles_per_b, 0, 0))


def _adaln_body(c_ref, w_ref, b_ref, o_ref):
    c = c_ref[...]
    o_ref[...] = jnp.dot(_silu(c).astype(BF16), w_ref[...], preferred_element_type=F32) + b_ref[...]


def adaln_all(c_all, w_ada_b, b_ada):
    L, R, tn = w_ada_b.shape[0], c_all.shape[0], 1536
    return pl.pallas_call(
        _adaln_body, grid=(L, 6 * D // tn), name="adaln",
        in_specs=[pl.BlockSpec((R, D), lambda l, j: (0, 0)),
                  pl.BlockSpec((None, D, tn), lambda l, j: (l, 0, j)),
                  pl.BlockSpec((None, 1, tn), lambda l, j: (l, 0, j))],
        out_specs=pl.BlockSpec((None, R, tn), lambda l, j: (l, 0, j)),
        out_shape=jax.ShapeDtypeStruct((L, R, 6 * D), F32),
        compiler_params=_cp(("parallel", "parallel"), 32),
    )(c_all, w_ada_b, b_ada.reshape(L, 1, 6 * D))


def _norm_win_body(x_ref, g_ref, sc_ref, sh_ref, w_ref, z_ref, h_scr):
    @pl.when(pl.program_id(1) == 0)
    def _():
        x = x_ref[...]
        y = x * lax.rsqrt(jnp.mean(x * x, axis=-1, keepdims=True) + EPS) * g_ref[...]
        h_scr[...] = (y * (1.0 + sc_ref[...]) + sh_ref[...]).astype(BF16)

    z_ref[...] = jnp.dot(h_scr[...], w_ref[...], preferred_element_type=F32)


def norm_win(x, g, sc, sh, w_in_b, l, per_token, tm, tiles_per_b):
    N, tn = x.shape[0], 512
    return pl.pallas_call(
        _norm_win_body, grid=(N // tm, D_IN // tn), name="norm_win",
        in_specs=[pl.BlockSpec((tm, D), lambda i, j: (i, 0)),
                  pl.BlockSpec((None, 1, D), lambda i, j: (l, 0, 0)),
                  _mod_spec(per_token, tm, tiles_per_b), _mod_spec(per_token, tm, tiles_per_b),
                  pl.BlockSpec((None, D, tn), lambda i, j: (l, 0, j))],
        out_specs=pl.BlockSpec((tm, tn), lambda i, j: (i, j)),
        out_shape=jax.ShapeDtypeStruct((N, D_IN), F32),
        scratch_shapes=[pltpu.VMEM((tm, D), BF16)],
        compiler_params=_cp(("parallel", "arbitrary"), 40),
    )(x, g, sc, sh, w_in_b)


def _pool_body(u_ref, wp_ref, ps_ref, o_ref, ext, *, tt):
    i = pl.program_id(1)

    @pl.when(i == 0)
    def _():
        ext[0:16, :] = jnp.zeros((16, W), F32)

    u = u_ref[...]
    ext[16:16 + tt, :] = u
    pos = (i * tt + lax.broadcasted_iota(jnp.int32, (tt, 1), 0)).astype(F32)
    for g, w in enumerate(POOL_WINDOWS):
        sl = slice(g * LANES, (g + 1) * LANES)
        acc = u[:, sl]
        for d in range(1, w):
            acc = acc + ext[16 - d:16 - d + tt, sl]
        pooled = acc / jnp.minimum(float(w), pos + 1.0) - u[:, sl]
        mixed = jnp.dot(pooled.astype(BF16), wp_ref[g], preferred_element_type=F32)
        o_ref[:, sl] = (mixed * ps_ref[:, sl]).astype(BF16)
    ext[0:16, :] = ext[tt:tt + 16, :]


def pool_prompt(z3, w_pool_b, pool_scale, l, tt=512):
    B, T = z3.shape[0], z3.shape[1]
    return pl.pallas_call(
        functools.partial(_pool_body, tt=tt), grid=(B, T // tt), name="pool_prompt",
        in_specs=[pl.BlockSpec((None, tt, W), lambda b, i: (b, i, 0)),
                  pl.BlockSpec((None, 4, LANES, LANES), lambda b, i: (l, 0, 0, 0)),
                  pl.BlockSpec((None, 1, W), lambda b, i: (l, 0, 0))],
        out_specs=pl.BlockSpec((None, tt, W), lambda b, i: (b, i, 0)),
        out_shape=jax.ShapeDtypeStruct((B, T, W), BF16),
        scratch_shapes=[pltpu.VMEM((16 + tt, W), F32)],
        compiler_params=_cp(("parallel", "arbitrary"), 32),
    )(z3, w_pool_b, pool_scale)


def _pool_dec_body(ue_ref, wp_ref, ps_ref, o_ref, *, n_new):
    for t in range(n_new):
        x_tok = ue_ref[16 + t]
        for g, w in enumerate(POOL_WINDOWS):
            sl = slice(g * LANES, (g + 1) * LANES)
            acc = x_tok[:, sl]
            for d in range(1, w):
                acc = acc + ue_ref[16 + t - d, :, sl]
            pooled = acc / float(w) - x_tok[:, sl]
            mixed = jnp.dot(pooled.astype(BF16), wp_ref[g], preferred_element_type=F32)
            o_ref[t, :, sl] = (mixed * ps_ref[:, sl]).astype(BF16)


def pool_sample(ue_t, w_pool_b, pool_scale, l, n_new):
    S = ue_t.shape[1]
    return pl.pallas_call(
        functools.partial(_pool_dec_body, n_new=n_new), grid=(1,), name="pool_sample",
        in_specs=[pl.BlockSpec((24, S, W), lambda i: (0, 0, 0)),
                  pl.BlockSpec((None, 4, LANES, LANES), lambda i: (l, 0, 0, 0)),
                  pl.BlockSpec((None, 1, W), lambda i: (l, 0, 0))],
        out_specs=pl.BlockSpec((n_new, S, W), lambda i: (0, 0, 0)),
        out_shape=jax.ShapeDtypeStruct((n_new, S, W), BF16),
        compiler_params=_cp(("arbitrary",), 32),
    )(ue_t, w_pool_b, pool_scale)


def _sb_body(bias_ref, q_ref, k_ref, v_ref, u_ref, o_ref, kb, vb, *, tq):
    hp, qi = pl.program_id(1), pl.program_id(2)

    @pl.when(qi == 0)
    def _():
        kb[...] = k_ref[...].astype(BF16)
        vb[...] = v_ref[...].astype(BF16)

    lane = lax.broadcasted_iota(jnp.int32, (1, LANES), 1)
    row = lax.broadcasted_iota(jnp.int32, (tq, tq), 0)
    col = lax.broadcasted_iota(jnp.int32, (tq, tq), 1)
    causal = col < row
    q = q_ref[...] * SB_SCALE
    upper = u_ref[...]
    outs = []
    for hh in range(2):
        qh = jnp.where(lane // 64 == hh, q, 0.0).astype(BF16)
        bias = bias_ref[2 * hp + hh]

        def scores(kbi, qh=qh, bias=bias):
            r0 = pl.multiple_of(kbi * tq, tq)
            z = lax.dot_general(qh, kb[pl.ds(r0, tq), :], _NT, preferred_element_type=F32) + bias
            return z, r0

        z, r0 = scores(qi)
        sp = _softplus(z)
        lk = jnp.where(causal, -sp, 0.0)
        between = jnp.dot(lk.astype(BF16), upper, preferred_element_type=F32)
        p = jnp.where(causal, jnp.exp(z - sp + between), 0.0)
        acc = jnp.dot(p.astype(BF16), vb[pl.ds(r0, tq), :], preferred_element_type=F32)
        carry = jnp.sum(lk, axis=1, keepdims=True)

        def body(j, c, scores=scores):
            acc, carry = c
            z, r0 = scores(qi - 1 - j)
            sp = _softplus(z)
            lk = -sp
            between = jnp.dot(lk.astype(BF16), upper, preferred_element_type=F32)
            p = jnp.exp(z - sp + between + carry)
            acc = acc + jnp.dot(p.astype(BF16), vb[pl.ds(r0, tq), :], preferred_element_type=F32)
            return acc, carry + jnp.sum(lk, axis=1, keepdims=True)

        acc, carry = lax.fori_loop(0, qi, body, (acc, carry))
        outs.append(acc)
    o_ref[...] = jnp.where(lane < 64, outs[0], outs[1]).astype(BF16)


def sb_prompt(z3, sb_bias_l, upper, tq=256):
    B, T = z3.shape[0], z3.shape[1]
    grid_spec = pltpu.PrefetchScalarGridSpec(
        num_scalar_prefetch=1, grid=(B, SB_HEADS // 2, T // tq),
        in_specs=[pl.BlockSpec((None, tq, LANES), lambda b, h, i, s: (b, i, 4 + h)),
                  pl.BlockSpec((None, T, LANES), lambda b, h, i, s: (b, 0, 8 + h)),
                  pl.BlockSpec((None, T, LANES), lambda b, h, i, s: (b, 0, 12 + h)),
                  pl.BlockSpec((tq, tq), lambda b, h, i, s: (0, 0))],
        out_specs=pl.BlockSpec((None, tq, LANES), lambda b, h, i, s: (b, i, h)),
        scratch_shapes=[pltpu.VMEM((T, LANES), BF16), pltpu.VMEM((T, LANES), BF16)])
    return pl.pallas_call(
        functools.partial(_sb_body, tq=tq), grid_spec=grid_spec, name="sb_prompt",
        out_shape=jax.ShapeDtypeStruct((B, T, W), BF16),
        compiler_params=_cp(("parallel", "parallel", "arbitrary"), 40),
    )(sb_bias_l, z3, z3, z3, upper)


def _sb_dec_body(pt_ref, bias_ref, qbd_ref, kn_ref, vn_ref, u_ref, un_ref, *rest, n_pages, n_new):
    k_pages, v_pages, o_ref = rest[:n_pages], rest[n_pages:2 * n_pages], rest[2 * n_pages]
    qbd = qbd_ref[...]
    bias = bias_ref[...]
    R = qbd.shape[0]

    def block(kblk, vblk, upper, carry, mask):
        z = lax.dot_general(qbd, kblk.astype(BF16), _NT, preferred_element_type=F32) + bias
        sp = _softplus(z)
        lk = -sp if mask is None else jnp.where(mask, -sp, 0.0)
        between = jnp.dot(lk.astype(upper.dtype), upper, preferred_element_type=F32)
        p = jnp.exp(z - sp + between + carry)
        if mask is not None:
            p = jnp.where(mask, p, 0.0)
        o = jnp.dot(p.astype(BF16), vblk.astype(BF16), preferred_element_type=F32)
        return o, carry + jnp.sum(lk, axis=1, keepdims=True)

    n_pad = kn_ref.shape[0]
    t_of_row = lax.broadcasted_iota(jnp.int32, (R, n_pad), 0) % n_new
    s_of_col = lax.broadcasted_iota(jnp.int32, (R, n_pad), 1)
    acc, carry = block(kn_ref[...], vn_ref[...], un_ref[...], jnp.zeros((R, 1), F32),
                       s_of_col < t_of_row)
    upper = u_ref[...]
    for p in reversed(range(n_pages)):
        o, carry = block(k_pages[p][...], v_pages[p][...], upper, carry, None)
        acc = acc + o
    o_ref[...] = acc


def sb_decode(page_table, bias_col, qbd, kn, vn, upper, upper_new, cache_k4, cache_v4, l, n_new):
    S, R = qbd.shape[0], qbd.shape[1]
    n_pages, page = page_table.shape[1], cache_k4.shape[2]
    n_pad = kn.shape[1]
    pt_flat = page_table.reshape(-1)

    def page_spec(p):
        return pl.BlockSpec((None, None, page, W), lambda s, pt: (l, pt[s * n_pages + p], 0, 0))

    grid_spec = pltpu.PrefetchScalarGridSpec(
        num_scalar_prefetch=1, grid=(S,),
        in_specs=[pl.BlockSpec((None, R, 1), lambda s, pt: (l, 0, 0)),
                  pl.BlockSpec((None, R, W), lambda s, pt: (s, 0, 0)),
                  pl.BlockSpec((None, n_pad, W), lambda s, pt: (s, 0, 0)),
                  pl.BlockSpec((None, n_pad, W), lambda s, pt: (s, 0, 0)),
                  pl.BlockSpec((page, page), lambda s, pt: (0, 0)),
                  pl.BlockSpec((n_pad, n_pad), lambda s, pt: (0, 0))]
        + [page_spec(p) for p in range(n_pages)] + [page_spec(p) for p in range(n_pages)],
        out_specs=pl.BlockSpec((None, R, W), lambda s, pt: (s, 0, 0)))
    return pl.pallas_call(
        functools.partial(_sb_dec_body, n_pages=n_pages, n_new=n_new), grid_spec=grid_spec,
        name="sb_decode", out_shape=jax.ShapeDtypeStruct((S, R, W), F32),
        compiler_params=_cp(("parallel",), 48),
    )(pt_flat, bias_col, qbd, kn, vn, upper, upper_new, *([cache_k4] * n_pages), *([cache_v4] * n_pages))


def _hgrn_body(*refs, nb, tt, C, valid, has_state):
    if has_state:
        (zq_ref, zf_ref, zi_ref, zg_ref, loglb_ref, l1m_ref, oml_ref, gn_ref, lt_ref, s0_ref,
         o_ref, so_ref, st, sq, skk, sb, si, sacc) = refs
    else:
        (zq_ref, zf_ref, zi_ref, zg_ref, loglb_ref, l1m_ref, oml_ref, gn_ref, lt_ref,
         o_ref, so_ref, st, sq, skk, sb, si, sacc) = refs
    i = pl.program_id(1)
    if has_state:
        for j in range(nb):
            for h in range(HG_HEADS):
                st[j * HG_HEADS + h] = s0_ref[j, h].T
    else:
        @pl.when(i == 0)
        def _():
            st[...] = jnp.zeros(st.shape, F32)

    loglb, l1m, oml, gn, lt = loglb_ref[...], l1m_ref[...], oml_ref[...], gn_ref[...], lt_ref[...]
    rowi = lax.broadcasted_iota(jnp.int32, (C, 1), 0)
    for j in range(nb):
        def chunk(c, carry, j=j):
            r0 = pl.multiple_of(c * C, C)
            zq = zq_ref[j, pl.ds(r0, C), :]
            zf = zf_ref[j, pl.ds(r0, C), :]
            zi = zi_ref[j, pl.ds(r0, C), :]
            zg = zg_ref[j, pl.ds(r0, C), :]
            q = _silu(zq)
            cc = l1m + (jnp.minimum(zf, 0.0) - jnp.log1p(jnp.exp(-jnp.abs(zf))))
            logf = jnp.maximum(loglb, cc) + jnp.log1p(jnp.exp(-jnp.abs(loglb - cc)))
            if valid < C:
                logf = jnp.where(rowi < valid, logf, 0.0)
            kk = oml * jax.nn.sigmoid(-zf)
            b = jnp.dot(lt, logf, precision=lax.Precision.HIGHEST, preferred_element_type=F32)
            sq[...] = q
            skk[...] = kk
            sb[...] = b
            si[...] = zi
            sacc[...] = jnp.zeros((C, W), F32)
            for d in range(valid):
                n = valid - d
                pw = sq[d:d + n, :] * skk[0:n, :] * jnp.exp(sb[d:d + n, :] - sb[0:n, :])
                i0 = si[0:n, :]
                parts = []
                for h in range(HG_HEADS):
                    sl = slice(h * HK, (h + 1) * HK)
                    parts.append(jnp.sum(pw[:, sl], axis=1, keepdims=True) * i0[:, sl])
                sacc[d:d + n, :] = sacc[d:d + n, :] + jnp.concatenate(parts, axis=1)
            gz = _silu(zg)
            outs = []
            for h in range(HG_HEADS):
                sl = slice(h * HK, (h + 1) * HK)
                bh = b[:, sl]
                st_h = st[j * HG_HEADS + h]
                qe = (q[:, sl] * jnp.exp(bh)).astype(BF16)
                oh = lax.dot_general(qe, st_h.astype(BF16), _NT, preferred_element_type=F32) + sacc[:, sl]
                on = oh * lax.rsqrt(jnp.mean(oh * oh, axis=1, keepdims=True) + EPS) * gn[:, sl]
                outs.append(on * gz[:, sl])
                b_last = sb[valid - 1:valid, sl]
                kd = (kk[:, sl] * jnp.exp(b_last - bh)).astype(BF16)
                upd = lax.dot_general(zi[:, sl].astype(BF16), kd, _TN, preferred_element_type=F32)
                st[j * HG_HEADS + h] = st_h * jnp.exp(b_last) + upd
            o_ref[j, pl.ds(r0, C), :] = jnp.concatenate(outs, axis=1).astype(o_ref.dtype)
            return carry

        lax.fori_loop(0, tt // C, chunk, 0)

    def write_state():
        for j in range(nb):
            for h in range(HG_HEADS):
                so_ref[j, h] = st[j * HG_HEADS + h].T

    if has_state:
        write_state()
    else:
        pl.when(i == pl.num_programs(1) - 1)(write_state)


def hgrn(zsrc, col0, hg_par, l, lt, s0, nb, tt, C, valid, out_dtype):
    B, T = zsrc.shape[0], zsrc.shape[1]
    has_state = s0 is not None
    zspec = lambda c: pl.BlockSpec((nb, tt, W), lambda b, i: (b, i, col0 + c))
    pspec = pl.BlockSpec((None, 1, W), lambda b, i: (l, 0, 0))
    sspec = pl.BlockSpec((nb, HG_HEADS, HK, HK), lambda b, i: (b, 0, 0, 0))
    in_specs = [zspec(0), zspec(1), zspec(2), zspec(3), pspec, pspec, pspec, pspec,
                pl.BlockSpec((C, C), lambda b, i: (0, 0))] + ([sspec] if has_state else [])
    args = [zsrc] * 4 + list(hg_par) + [lt] + ([s0] if has_state else [])
    return pl.pallas_call(
        functools.partial(_hgrn_body, nb=nb, tt=tt, C=C, valid=valid, has_state=has_state),
        grid=(B // nb, T // tt), name="hgrn_state" if has_state else "hgrn_prompt",
        in_specs=in_specs,
        out_specs=[pl.BlockSpec((nb, tt, W), lambda b, i: (b, i, 0)), sspec],
        out_shape=[jax.ShapeDtypeStruct((B, T, W), out_dtype),
                   jax.ShapeDtypeStruct((B, HG_HEADS, HK, HK), F32)],
        scratch_shapes=[pltpu.VMEM((nb * HG_HEADS, HK, HK), F32)] + [pltpu.VMEM((C, W), F32)] * 5,
        compiler_params=_cp(("parallel", "arbitrary"), 40),
    )(*args)


def _mix_body(bp_ref, bs_ref, bh_ref, g0_ref, g1_ref, g2_ref, x_ref, ga_ref, wb_ref, wo_ref,
              gf_ref, sc_ref, sh_ref, wr_ref, br_ref, lt_ref, run0_ref,
              xo_ref, h2_ref, idx_ref, wgt_ref, rank_ref, cnt_ref, run):
    i = pl.program_id(0)

    @pl.when(i == 0)
    def _():
        run[...] = run0_ref[...]

    merged = jax.nn.sigmoid(g0_ref[...]) * jnp.dot(bp_ref[...], wb_ref[0], preferred_element_type=F32)
    merged = merged + jax.nn.sigmoid(g1_ref[...]) * jnp.dot(bs_ref[...], wb_ref[1], preferred_element_type=F32)
    merged = merged + jax.nn.sigmoid(g2_ref[...]) * jnp.dot(bh_ref[...], wb_ref[2], preferred_element_type=F32)
    mix = jnp.dot(merged.astype(BF16), wo_ref[...], preferred_element_type=F32)
    x = x_ref[...] + ga_ref[...] * mix
    xo_ref[...] = x
    y = x * lax.rsqrt(jnp.mean(x * x, axis=-1, keepdims=True) + EPS) * gf_ref[...]
    h2 = (y * (1.0 + sc_ref[...]) + sh_ref[...]).astype(BF16)
    h2_ref[...] = h2

    tm = x.shape[0]
    lane = lax.broadcasted_iota(jnp.int32, (tm, LANES), 1).astype(F32)
    logits = jnp.dot(h2, wr_ref[...], preferred_element_type=F32) + br_ref[...]
    cur = jnp.where(lane < float(N_EXP), logits, NEG)
    vals, idxs = [], []
    for _ in range(TOP_K):
        m = jnp.max(cur, axis=1, keepdims=True)
        ik = jnp.min(jnp.where(cur == m, lane, float(LANES)), axis=1, keepdims=True)
        vals.append(m)
        idxs.append(ik)
        cur = jnp.where(lane == ik, NEG, cur)
    es = [jnp.exp(v - vals[0]) for v in vals]
    denom = es[0] + es[1] + es[2] + es[3]
    hot = jnp.zeros((tm, LANES), F32)
    for ik in idxs:
        hot = hot + jnp.where(lane == ik, 1.0, 0.0)
    base = jnp.dot(lt_ref[...], hot.astype(BF16), preferred_element_type=F32) + run[...]
    idx_o = jnp.zeros((tm, LANES), F32)
    wgt_o = jnp.zeros((tm, LANES), F32)
    rank_o = jnp.zeros((tm, LANES), F32)
    for k in range(TOP_K):
        rk = jnp.sum(jnp.where(lane == idxs[k], base, 0.0), axis=1, keepdims=True)
        idx_o = jnp.where(lane == float(k), idxs[k], idx_o)
        wgt_o = jnp.where(lane == float(k), es[k] / denom, wgt_o)
        rank_o = jnp.where(lane == float(k), rk, rank_o)
    idx_ref[...] = idx_o.astype(jnp.int32)
    wgt_ref[...] = wgt_o
    rank_ref[...] = rank_o.astype(jnp.int32)
    run[...] = run[...] + jnp.sum(hot, axis=0, keepdims=True)
    cnt_ref[...] = run[...]


def mix_out(br_pool, br_sb, br_hg, z, x, ga, sc2, sh2, wts, l, run0, lt, per_token, tm, tiles_per_b):
    N = x.shape[0]
    w_branch_b, w_out_b, g_ffn, w_router_p, b_router_p = wts
    row = lambda w: pl.BlockSpec((tm, w), lambda i: (i, 0))
    gate = lambda c: pl.BlockSpec((tm, D), lambda i: (i, 4 + c))
    mod = _mod_spec(per_token, tm, tiles_per_b)
    par = lambda w: pl.BlockSpec((None, 1, w), lambda i: (l, 0, 0))
    small = pl.BlockSpec((tm, LANES), lambda i: (i, 0))
    return pl.pallas_call(
        _mix_body, grid=(N // tm,), name="mix_out",
        in_specs=[row(W), row(W), row(W), gate(0), gate(1), gate(2), row(D), mod,
                  pl.BlockSpec((None, 3, W, D), lambda i: (l, 0, 0, 0)),
                  pl.BlockSpec((None, D, D), lambda i: (l, 0, 0)),
                  par(D), mod, mod,
                  pl.BlockSpec((None, D, LANES), lambda i: (l, 0, 0)), par(LANES),
                  pl.BlockSpec((tm, tm), lambda i: (0, 0)),
                  pl.BlockSpec((1, LANES), lambda i: (0, 0))],
        out_specs=[row(D), row(D), small, small, small, pl.BlockSpec((1, LANES), lambda i: (0, 0))],
        out_shape=[jax.ShapeDtypeStruct((N, D), F32), jax.ShapeDtypeStruct((N, D), BF16),
                   jax.ShapeDtypeStruct((N, LANES), jnp.int32), jax.ShapeDtypeStruct((N, LANES), F32),
                   jax.ShapeDtypeStruct((N, LANES), jnp.int32), jax.ShapeDtypeStruct((1, LANES), F32)],
        scratch_shapes=[pltpu.VMEM((1, LANES), F32)],
        compiler_params=_cp(("arbitrary",), 48),
    )(br_pool, br_sb, br_hg, z, z, z, x, ga, w_branch_b, w_out_b, g_ffn, sc2, sh2,
      w_router_p, b_router_p, lt, run0)


def _moe_body(be_ref, na_ref, x_ref, wgu_ref, bgu_ref, wdn_ref, bdn_ref, y_ref, wgu_b, wdn_b):
    i = pl.program_id(0)
    active = i < na_ref[0]

    @pl.when(active)
    def _():
        changed = jnp.logical_or(i == 0, be_ref[i] != be_ref[jnp.maximum(i - 1, 0)])

        @pl.when(changed)
        def _():
            def cast(r, c):
                r0 = pl.multiple_of(r * 128, 128)
                wgu_b[pl.ds(r0, 128), :] = wgu_ref[pl.ds(r0, 128), :].astype(BF16)
                wdn_b[pl.ds(r0, 128), :] = wdn_ref[pl.ds(r0, 128), :].astype(BF16)
                return c
            lax.fori_loop(0, D // 128, cast, 0)

        gu = jnp.dot(x_ref[...], wgu_b[...], preferred_element_type=F32) + bgu_ref[...]
        gate = jnp.minimum(gu[:, :D_FF], SWIGLU_LIMIT)
        up = jnp.clip(gu[:, D_FF:], -SWIGLU_LIMIT, SWIGLU_LIMIT)
        act = (up + 1.0) * gate * jax.nn.sigmoid(SWIGLU_ALPHA * gate)
        y_ref[...] = jnp.dot(act.astype(BF16), wdn_b[...], preferred_element_type=F32) + bdn_ref[...]

    @pl.when(jnp.logical_not(active))
    def _():
        y_ref[...] = jnp.zeros(y_ref.shape, F32)


def moe_experts(blk_expert, n_active, x_rows, w_gu, b_gu, w_dn, b_dn, l):
    n_rows = x_rows.shape[0]
    grid_spec = pltpu.PrefetchScalarGridSpec(
        num_scalar_prefetch=2, grid=(n_rows // MOE_TM,),
        in_specs=[pl.BlockSpec((MOE_TM, D), lambda i, be, na: (i, 0)),
                  pl.BlockSpec((None, None, D, 2 * D_FF), lambda i, be, na: (l, be[i], 0, 0)),
                  pl.BlockSpec((None, None, 1, 2 * D_FF), lambda i, be, na: (l, be[i], 0, 0)),
                  pl.BlockSpec((None, None, D_FF, D), lambda i, be, na: (l, be[i], 0, 0)),
                  pl.BlockSpec((None, None, 1, D), lambda i, be, na: (l, be[i], 0, 0))],
        out_specs=pl.BlockSpec((MOE_TM, D), lambda i, be, na: (i, 0)),
        scratch_shapes=[pltpu.VMEM((D, 2 * D_FF), BF16), pltpu.VMEM((D_FF, D), BF16)])
    return pl.pallas_call(
        _moe_body, grid_spec=grid_spec, name="moe_experts",
        out_shape=jax.ShapeDtypeStruct((n_rows, D), F32),
        compiler_params=_cp(("arbitrary",), 56),
    )(blk_expert, n_active, x_rows, w_gu, b_gu, w_dn, b_dn)


def _combine_body(yg_ref, w_ref, x_ref, ga_ref, gfin_ref, o_ref, *, final):
    w = w_ref[...]
    moe = yg_ref[0] * w[:, 0:1]
    for k in range(1, TOP_K):
        moe = moe + yg_ref[k] * w[:, k:k + 1]
    x = x_ref[...] + ga_ref[...] * moe
    if final:
        x = x * lax.rsqrt(jnp.mean(x * x, axis=-1, keepdims=True) + EPS) * gfin_ref[...]
    o_ref[...] = x


def combine(yg, wgt, x, ga, g_final, row0, per_token, tm, tiles_per_b, final):
    N, off = x.shape[0], row0 // tm
    return pl.pallas_call(
        functools.partial(_combine_body, final=final), grid=(N // tm,), name="combine",
        in_specs=[pl.BlockSpec((TOP_K, tm, D), lambda i: (0, off + i, 0)),
                  pl.BlockSpec((tm, LANES), lambda i: (off + i, 0)),
                  pl.BlockSpec((tm, D), lambda i: (i, 0)),
                  _mod_spec(per_token, tm, tiles_per_b),
                  pl.BlockSpec((1, D), lambda i: (0, 0))],
        out_specs=pl.BlockSpec((tm, D), lambda i: (i, 0)),
        out_shape=jax.ShapeDtypeStruct((N, D), F32),
        compiler_params=_cp(("parallel",), 40),
    )(yg, wgt, x, ga, g_final)


def kernel(x_prompt, x_sample, cache_k, cache_v, state_pool, state_hgrn, page_table, c_prompt, c_sample,
           g_mix, g_ffn, g_final, w_ada, b_ada, w_in, w_pool, pool_scale, sb_bias, lb_logits, hg_norm,
           w_branch, w_out, w_router, b_router, w_gu, b_gu, w_dn, b_dn):
    B, T, _ = x_prompt.shape
    S, TN, _ = x_sample.shape
    L = w_in.shape[0]
    Np, Ns = B * T, S * TN
    n_phys, page = cache_k.shape[1], cache_k.shape[2]

    w_ada_b, w_in_b, w_pool_b = w_ada.astype(BF16), w_in.astype(BF16), w_pool.astype(BF16)
    w_branch_b, w_out_b = w_branch.astype(BF16), w_out.astype(BF16)
    w_router_p = jnp.pad(w_router, ((0, 0), (0, 0), (0, LANES - N_EXP))).astype(BF16)
    b_router_p = jnp.pad(b_router, ((0, 0), (0, LANES - N_EXP))).reshape(L, 1, LANES)
    lb = jnp.cumsum(jax.nn.softmax(lb_logits.astype(F32), axis=0), axis=0)
    lb = jnp.maximum(lb - lb[0], 0.0)
    hg_pars = (jnp.log(lb).reshape(L, 1, W), jnp.log1p(-lb).reshape(L, 1, W), (1.0 - lb).reshape(L, 1, W),
               jnp.tile(hg_norm, (1, HG_HEADS)).reshape(L, 1, W))
    pool_scale3 = pool_scale.reshape(L, 1, W)
    g_mix3, g_ffn3 = g_mix.reshape(L, 1, D), g_ffn.reshape(L, 1, D)
    g_final2 = g_final.reshape(1, D)
    b_gu4, b_dn4 = b_gu.reshape(L, N_EXP, 1, 2 * D_FF), b_dn.reshape(L, N_EXP, 1, D)
    cache_k4 = cache_k.reshape(L, n_phys, page, W)
    cache_v4 = cache_v.reshape(L, n_phys, page, W)

    def tri(n, strict_upper):
        r = lax.broadcasted_iota(jnp.int32, (n, n), 0)
        c = lax.broadcasted_iota(jnp.int32, (n, n), 1)
        return (r > c) if strict_upper else (r >= c)

    upper256 = tri(256, True).astype(BF16)
    upper_pg = tri(page, True).astype(BF16)
    lower256 = upper256
    upper16 = tri(16, True).astype(F32)
    lt16 = tri(16, False).astype(F32)
    lt8 = tri(8, False).astype(F32)
    bias_col = jnp.repeat(sb_bias, TN, axis=1).reshape(L, SB_HEADS * TN, 1)
    head_eye = jnp.eye(SB_HEADS, dtype=F32)

    n_c = B + S
    c_all = jnp.pad(jnp.concatenate([c_prompt, c_sample], axis=0), ((0, (-n_c) % 8), (0, 0)))
    mod = adaln_all(c_all, w_ada_b, b_ada)
    mod_p = mod[:, :B].reshape(L, B, 6, D)
    mod_s = mod[:, B:B + S].reshape(L, S, 6, D)

    xp = x_prompt.reshape(Np, D)
    xs = x_sample.reshape(Ns, D)
    tm_p, tm_s = 256, 256
    tpb = T // tm_p
    zero_run = jnp.zeros((1, LANES), F32)
    tok_ids = jnp.repeat(jnp.arange(Np + Ns, dtype=jnp.int32), TOP_K)
    n_rows = -(-((Np + Ns) * TOP_K + N_EXP * (MOE_TM - 1)) // MOE_TM) * MOE_TM
    outs = {k: [] for k in ("kp", "vp", "pp", "hp", "ks", "vs", "ps", "hs")}

    for l in range(L):
        mp = [mod_p[l, :, j].reshape(B, 1, D) for j in range(6)]
        ms = [jnp.repeat(mod_s[l, :, j], TN, axis=0).reshape(1, Ns, D) for j in range(6)]
        wts = (w_branch_b, w_out_b, g_ffn3, w_router_p, b_router_p)

        zp = norm_win(xp, g_mix3, mp[1], mp[0], w_in_b, l, False, 512, T // 512)
        zp3 = zp.reshape(B, T, D_IN)
        br_pool = pool_prompt(zp3, w_pool_b, pool_scale3, l)
        br_sb = sb_prompt(zp3, sb_bias[l], upper256)
        br_hg, s_new = hgrn(zp3, 4, [p for p in hg_pars], l, lt16, None, 1, 512, 16, 16, BF16)
        outs["kp"].append(zp3[:, :, 2 * W:3 * W].reshape(B, T, SB_HEADS, 64))
        outs["vp"].append(zp3[:, :, 3 * W:4 * W].reshape(B, T, SB_HEADS, 64))
        outs["pp"].append(zp3[:, T - POOL_BUF:, :W])
        outs["hp"].append(s_new)
        xp, h2p, idxp, wgtp, rankp, cntp = mix_out(
            br_pool.reshape(Np, W), br_sb.reshape(Np, W), br_hg.reshape(Np, W), zp, xp,
            mp[2], mp[4], mp[3], wts, l, zero_run, lower256, False, tm_p, tpb)

        zs = norm_win(xs, g_mix3, ms[1], ms[0], w_in_b, l, True, Ns, 1)
        zs3 = zs.reshape(S, TN, D_IN)
        u_ext = jnp.concatenate([state_pool[l], zs3[:, :, :W]], axis=1)
        ue_t = jnp.pad(u_ext, ((0, 0), (1, 24 - 1 - POOL_BUF - TN), (0, 0))).transpose(1, 0, 2)
        br_pool_s = pool_sample(ue_t, w_pool_b, pool_scale3, l, TN).transpose(1, 0, 2)
        q_s = zs3[:, :, W:2 * W].reshape(S, TN, SB_HEADS, 64) * SB_SCALE
        qbd = jnp.einsum("sthd,gh->sgthd", q_s, head_eye).reshape(S, SB_HEADS * TN, W).astype(BF16)
        k_new, v_new = zs3[:, :, 2 * W:3 * W], zs3[:, :, 3 * W:4 * W]
        pad16 = lambda a: jnp.pad(a, ((0, 0), (0, 16 - TN), (0, 0)))
        o_dec = sb_decode(page_table, bias_col, qbd, pad16(k_new), pad16(v_new), upper_pg, upper16,
                          cache_k4, cache_v4, l, TN)
        o_dec = o_dec.reshape(S, SB_HEADS, TN, SB_HEADS, 64)
        br_sb_s = jnp.einsum("sgthd,gh->sthd", o_dec, head_eye).reshape(Ns, W).astype(BF16)
        zhg = jnp.pad(zs3[:, :, 4 * W:8 * W], ((0, 0), (0, 8 - TN), (0, 0)))
        br_hg_s, s_new_s = hgrn(zhg, 0, [p for p in hg_pars], l, lt8, state_hgrn[l], 4, 8, 8, TN, F32)
        br_hg_s = br_hg_s[:, :TN].reshape(Ns, W).astype(BF16)
        outs["ks"].append(k_new.reshape(S, TN, SB_HEADS, 64))
        outs["vs"].append(v_new.reshape(S, TN, SB_HEADS, 64))
        outs["ps"].append(u_ext[:, -POOL_BUF:])
        outs["hs"].append(s_new_s)
        xs, h2s, idxs, wgts, ranks, cnts = mix_out(
            br_pool_s.reshape(Ns, W), br_sb_s, br_hg_s, zs, xs,
            ms[2], ms[4], ms[3], wts, l, cntp, lower256, True, tm_s, 1)

        idx_all = jnp.concatenate([idxp[:, :TOP_K], idxs[:, :TOP_K]], axis=0)
        rank_all = jnp.concatenate([rankp[:, :TOP_K], ranks[:, :TOP_K]], axis=0)
        wgt_all = jnp.concatenate([wgtp, wgts], axis=0)
        counts = cnts[0, :N_EXP].astype(jnp.int32)
        padded = (counts + MOE_TM - 1) // MOE_TM * MOE_TM
        pad_end = jnp.cumsum(padded)
        pad_start = pad_end - padded
        dest = pad_start[idx_all] + rank_all
        n_blocks = n_rows // MOE_TM
        blk_expert = jnp.minimum(
            jnp.searchsorted(pad_end, jnp.arange(n_blocks, dtype=jnp.int32) * MOE_TM, side="right"),
            N_EXP - 1).astype(jnp.int32)
        n_active = (pad_end[-1:] // MOE_TM).astype(jnp.int32)
        row_tok = jnp.zeros((n_rows,), jnp.int32).at[dest.reshape(-1)].set(tok_ids)
        h2_all = jnp.concatenate([h2p, h2s], axis=0)
        x_rows = h2_all[row_tok]
        y_rows = moe_experts(blk_expert, n_active, x_rows, w_gu, b_gu4, w_dn, b_dn4, l)
        yg = y_rows[dest.T]
        final = l == L - 1
        xp = combine(yg, wgt_all, xp, mp[5], g_final2, 0, False, tm_p, tpb, final)
        xs = combine(yg, wgt_all, xs, ms[5], g_final2, Np, True, tm_s, 1, final)

    st = lambda k: jnp.stack(outs[k])
    return (xp.reshape(B, T, D), xs.reshape(S, TN, D),
            st("kp"), st("vp"), st("pp"), st("hp"), st("ks"), st("vs"), st("ps"), st("hs"))
```
